```python
import jax, jax.numpy as jnp
from jax import lax
import numpy as np

D_MODEL = 1024
BATCH = 8
SEQ = 8192
DEPTH = 2

POOL_WINDOWS = (2, 4, 8, 16)
POOL_GROUPS = 4
WIDTH_A = D_MODEL // 2
POOL_GROUP_DIM = WIDTH_A // POOL_GROUPS
CHUNK = 128
SGU_HEADS = 4
WIDTH_B = D_MODEL // 2
SGU_HEAD_DIM = WIDTH_B // SGU_HEADS
WIDTH_C = D_MODEL // 2
CONV_WIDTH = 3
N_BRANCHES = 3
W_IN_COLS = WIDTH_A + 2 * WIDTH_B + 3 * WIDTH_C + N_BRANCHES * D_MODEL
D_FF = 2816
EPS = 1e-6

kernel_name = "hybrid_pool_sgu_shortconv_gated_block"


def rmsnorm(x, g):
    xf = x.astype(jnp.float32)
    y = xf * lax.rsqrt(jnp.mean(xf * xf, axis=-1, keepdims=True) + EPS)
    return (y * g.astype(jnp.float32)).astype(x.dtype)


def causal_dwconv3(z, w):
    s = z.shape[1]
    zp = jnp.pad(z, ((0, 0), (CONV_WIDTH - 1, 0), (0, 0)))
    return zp[:, :s] * w[0] + zp[:, 1:s + 1] * w[1] + zp[:, 2:s + 2] * w[2]


def multiscale_pool(a):
    bsz, s = a.shape[0], a.shape[1]
    af = a.astype(jnp.float32)
    cs = jnp.cumsum(af, axis=1)
    cs_pad = jnp.pad(cs, ((0, 0), (1, 0), (0, 0), (0, 0)))
    t = jnp.arange(s, dtype=jnp.float32)
    outs = []
    for g, w in enumerate(POOL_WINDOWS):
        upper = cs[:, :, g]
        lower = jnp.concatenate(
            [jnp.zeros((bsz, w - 1, a.shape[3]), jnp.float32), cs_pad[:, :s - w + 1, g]], axis=1)
        cnt = jnp.minimum(t + 1.0, float(w))[None, :, None]
        outs.append((upper - lower) / cnt - af[:, :, g])
    return jnp.stack(outs, axis=2).astype(a.dtype)


def setup_inputs(seed: int = 0) -> dict:
    key = jax.random.key(seed)
    k = jax.random.split(key, 20)
    n = jax.random.normal
    f32 = jnp.float32
    L = DEPTH
    tri = jnp.tril(jnp.ones((CHUNK, CHUNK), f32))
    row_scale = (jnp.arange(CHUNK, dtype=f32) + 1.0) ** -0.5
    w_spatial = n(k[5], (L, SGU_HEADS, CHUNK, CHUNK), f32) * tri * row_scale[:, None]
    return {
        "x": n(k[0], (BATCH, SEQ, D_MODEL), f32),
        "g_mix": 1.0 + 0.02 * n(k[1], (L, D_MODEL), f32),
        "w_in": n(k[2], (L, D_MODEL, W_IN_COLS), f32) * D_MODEL ** -0.5,
        "w_pool": n(k[3], (L, POOL_GROUPS, POOL_GROUP_DIM, POOL_GROUP_DIM), f32) * POOL_GROUP_DIM ** -0.5,
        "pool_scale": 1.0 + 0.1 * n(k[4], (L, WIDTH_A), f32),
        "g_sgu": 1.0 + 0.02 * n(k[6], (L, WIDTH_B), f32),
        "w_spatial": w_spatial,
        "b_spatial": 1.0 + 0.01 * n(k[7], (L, SGU_HEADS, CHUNK), f32),
        "conv_c": n(k[8], (L, CONV_WIDTH, WIDTH_C), f32) * CONV_WIDTH ** -0.5,
        "w_branch_a": n(k[9], (L, WIDTH_A, D_MODEL), f32) * WIDTH_A ** -0.5,
        "w_branch_b": n(k[10], (L, WIDTH_B, D_MODEL), f32) * WIDTH_B ** -0.5,
        "w_branch_c": n(k[11], (L, WIDTH_C, D_MODEL), f32) * WIDTH_C ** -0.5,
        "w_o": n(k[12], (L, D_MODEL, D_MODEL), f32) * D_MODEL ** -0.5,
        "g_ffn": 1.0 + 0.02 * n(k[13], (L, D_MODEL), f32),
        "w_up": n(k[14], (L, D_MODEL, 2 * D_FF), f32) * D_MODEL ** -0.5,
        "conv_ffn": n(k[15], (L, CONV_WIDTH, 2 * D_FF), f32) * CONV_WIDTH ** -0.5,
        "conv_ffn_b": 0.01 * n(k[16], (L, 2 * D_FF), f32),
        "w_down": n(k[17], (L, D_FF, D_MODEL), f32) * D_FF ** -0.5,
        "g_final": 1.0 + 0.02 * n(k[18], (D_MODEL,), f32),
    }


def reference(x, g_mix, w_in, w_pool, pool_scale, g_sgu, w_spatial, b_spatial, conv_c,
              w_branch_a, w_branch_b, w_branch_c, w_o, g_ffn, w_up, conv_ffn, conv_ffn_b,
              w_down, g_final):
    bsz, s, _ = x.shape
    n_chunks = s // CHUNK
    splits = np.cumsum([WIDTH_A, 2 * WIDTH_B, WIDTH_C, WIDTH_C, WIDTH_C, D_MODEL, D_MODEL]).tolist()
    for l in range(DEPTH):
        h = rmsnorm(x, g_mix[l])
        p = h @ w_in[l]
        a, uv, c_b, c_c, c_x, ga, gb, gc = jnp.split(p, splits, axis=-1)

        a = a.reshape(bsz, s, POOL_GROUPS, POOL_GROUP_DIM)
        pa = multiscale_pool(a)
        ya = jnp.einsum("bsgd,gde->bsge", pa, w_pool[l]).reshape(bsz, s, WIDTH_A) * pool_scale[l]

        uv = jax.nn.gelu(uv)
        u, v = jnp.split(uv, 2, axis=-1)
        v = rmsnorm(v, g_sgu[l])
        v = v.reshape(bsz, n_chunks, CHUNK, SGU_HEADS, SGU_HEAD_DIM)
        ws = jnp.tril(w_spatial[l])
        sv = jnp.einsum("gts,bcsgd->bctgd", ws, v) + b_spatial[l].T[None, None, :, :, None]
        yb = u * sv.reshape(bsz, s, WIDTH_B)

        yc = c_b * causal_dwconv3(c_c * c_x, conv_c[l])

        merged = (jax.nn.sigmoid(ga) * (ya @ w_branch_a[l])
                  + jax.nn.sigmoid(gb) * (yb @ w_branch_b[l])
                  + jax.nn.sigmoid(gc) * (yc @ w_branch_c[l]))
        x = x + merged @ w_o[l]

        h = rmsnorm(x, g_ffn[l])
        up = causal_dwconv3(h @ w_up[l], conv_ffn[l]) + conv_ffn_b[l]
        gate, val = jnp.split(up, 2, axis=-1)
        x = x + (jax.nn.silu(gate) * val) @ w_down[l]
    return rmsnorm(x, g_final)
```

```python
import functools

import jax
import jax.numpy as jnp
from jax import lax
from jax.experimental import pallas as pl
from jax.experimental.pallas import tpu as pltpu

POOL_WINDOWS = (2, 4, 8, 16)
CHUNK = 128
SGU_HEADS = 4
CONV_WIDTH = 3
EPS = 1e-6

LANES = 128
SUBLANES = 8
POOL_HALO = 16
CONV_HALO = SUBLANES
FF_CHUNK = 256

TOKEN_TILE = 512
VMEM_LIMIT_BYTES = 56 * 1024 * 1024


def _dot(a, b):
    return jnp.dot(a, b, preferred_element_type=jnp.float32)


def _rmsnorm(x, g):
    ms = jnp.mean(x * x, axis=-1, keepdims=True)
    return x * lax.rsqrt(ms + EPS) * g


def _sigmoid(x):
    return 0.5 * jnp.tanh(0.5 * x) + 0.5


def _gelu_tanh(x):
    c = 0.7978845608028654
    return x * (0.5 * (1.0 + jnp.tanh(c * (x + 0.044715 * (x * x * x)))))


def _mixer_kernel(x_ref, gmix_ref, win_ref, wpool_ref, pscale_ref, gsgu_ref, ws_ref, bsp_ref,
                  convc_ref, wba_ref, wbb_ref, wbc_ref, wo_ref, out_ref,
                  hb_s, a_s, z_s, yb_s, *, tm, tiles_per_seq, width):
    f32, bf16 = jnp.float32, jnp.bfloat16
    wa = wb = wc = width
    d_model = x_ref.shape[1]
    seq_tile = lax.rem(pl.program_id(0), tiles_per_seq)

    @pl.when(seq_tile == 0)
    def _():
        a_s[0:POOL_HALO, :] = jnp.zeros((POOL_HALO, wa), f32)
        z_s[0:CONV_HALO, :] = jnp.zeros((CONV_HALO, wc), f32)

    x = x_ref[...]
    hb_s[...] = _rmsnorm(x, gmix_ref[...]).astype(bf16)

    o_a, o_u, o_v = 0, wa, wa + wb
    o_cb = wa + 2 * wb
    o_cc, o_cx = o_cb + wc, o_cb + 2 * wc
    o_ga = o_cb + 3 * wc
    o_gb, o_gc = o_ga + d_model, o_ga + 2 * d_model

    a_s[POOL_HALO:POOL_HALO + tm, :] = _dot(hb_s[...], win_ref[:, o_a:o_a + wa])
    pos = seq_tile * tm + lax.broadcasted_iota(jnp.int32, (tm, LANES), 0)
    ya_parts = []
    for g, w in enumerate(POOL_WINDOWS):
        lanes = slice(g * LANES, (g + 1) * LANES)
        cur = a_s[POOL_HALO:POOL_HALO + tm, lanes]
        s = cur
        for k in range(1, w):
            s = s + a_s[POOL_HALO - k:POOL_HALO - k + tm, lanes]
        cnt = jnp.minimum(pos + 1, w).astype(f32)
        pa = s / cnt - cur
        ya_parts.append(_dot(pa.astype(bf16), wpool_ref[g]))
    a_s[0:POOL_HALO, :] = a_s[tm:tm + POOL_HALO, :]
    ya = (jnp.concatenate(ya_parts, axis=1) * pscale_ref[...]).astype(bf16)
    merged = _sigmoid(_dot(hb_s[...], win_ref[:, o_ga:o_ga + d_model])) * _dot(ya, wba_ref[...])

    u = _gelu_tanh(_dot(hb_s[...], win_ref[:, o_u:o_u + wb]))
    v = _gelu_tanh(_dot(hb_s[...], win_ref[:, o_v:o_v + wb]))
    vn = _rmsnorm(v, gsgu_ref[...]).astype(bf16)
    row = lax.broadcasted_iota(jnp.int32, (CHUNK, CHUNK), 0)
    col = lax.broadcasted_iota(jnp.int32, (CHUNK, CHUNK), 1)
    head_dim = wb // SGU_HEADS
    for g in range(SGU_HEADS):
        ws_g = jnp.where(row >= col, ws_ref[g], 0.0).astype(bf16)
        bias_g = bsp_ref[g]
        lanes = slice(g * head_dim, (g + 1) * head_dim)
        for c in range(tm // CHUNK):
            rows = slice(c * CHUNK, (c + 1) * CHUNK)
            sv = _dot(ws_g, vn[rows, lanes]) + bias_g
            yb_s[rows, lanes] = (u[rows, lanes] * sv).astype(bf16)
    merged = merged + (_sigmoid(_dot(hb_s[...], win_ref[:, o_gb:o_gb + d_model]))
                       * _dot(yb_s[...], wbb_ref[...]))

    c_c = _dot(hb_s[...], win_ref[:, o_cc:o_cc + wc])
    c_x = _dot(hb_s[...], win_ref[:, o_cx:o_cx + wc])
    z_s[CONV_HALO:CONV_HALO + tm, :] = c_c * c_x
    conv = z_s[CONV_HALO:CONV_HALO + tm, :] * convc_ref[CONV_WIDTH - 1:CONV_WIDTH, :]
    for k in range(1, CONV_WIDTH):
        conv = conv + (z_s[CONV_HALO - k:CONV_HALO - k + tm, :]
                       * convc_ref[CONV_WIDTH - 1 - k:CONV_WIDTH - k, :])
    z_s[0:CONV_HALO, :] = z_s[tm:tm + CONV_HALO, :]
    yc = (_dot(hb_s[...], win_ref[:, o_cb:o_cb + wc]) * conv).astype(bf16)
    merged = merged + (_sigmoid(_dot(hb_s[...], win_ref[:, o_gc:o_gc + d_model]))
                       * _dot(yc, wbc_ref[...]))

    out_ref[...] = x_ref[...] + _dot(merged.astype(bf16), wo_ref[...])


def _ffn_kernel(x_ref, gffn_ref, wup_ref, convw_ref, convb_ref, wdown_ref, gfin_ref, out_ref,
                hb_s, up_s, carry_s, acc_s, *, tm, tiles_per_seq, final_norm):
    f32, bf16 = jnp.float32, jnp.bfloat16
    n_chunks = wdown_ref.shape[0] // FF_CHUNK
    seq_tile = lax.rem(pl.program_id(0), tiles_per_seq)

    @pl.when(seq_tile == 0)
    def _():
        carry_s[...] = jnp.zeros(carry_s.shape, f32)

    hb_s[...] = _rmsnorm(x_ref[...], gffn_ref[...]).astype(bf16)

    for j in range(n_chunks):
        cols = slice(j * 2 * FF_CHUNK, (j + 1) * 2 * FF_CHUNK)
        up_s[0:CONV_HALO, :] = carry_s[j]
        up_s[CONV_HALO:CONV_HALO + tm, :] = _dot(hb_s[...], wup_ref[:, cols])
        carry_s[j] = up_s[tm:tm + CONV_HALO, :]
        conv = (up_s[CONV_HALO:CONV_HALO + tm, :] * convw_ref[CONV_WIDTH - 1:CONV_WIDTH, cols]
                + convb_ref[:, cols])
        for k in range(1, CONV_WIDTH):
            conv = conv + (up_s[CONV_HALO - k:CONV_HALO - k + tm, :]
                           * convw_ref[CONV_WIDTH - 1 - k:CONV_WIDTH - k, cols])
        half_gate = 0.5 * conv[:, :FF_CHUNK]
        act = (half_gate * (1.0 + jnp.tanh(half_gate)) * conv[:, FF_CHUNK:]).astype(bf16)
        down = _dot(act, wdown_ref[j * FF_CHUNK:(j + 1) * FF_CHUNK, :])
        if j == 0:
            acc_s[...] = x_ref[...] + down
        else:
            acc_s[...] += down

    y = acc_s[...]
    if final_norm:
        y = _rmsnorm(y, gfin_ref[...])
    out_ref[...] = y


def _full(shape):
    return pl.BlockSpec(shape, lambda i: (0,) * len(shape))


def _mixer_call(x2d, gmix, win, wpool, pscale, gsgu, ws, bsp, convc, wba, wbb, wbc, wo,
                *, tm, tiles_per_seq):
    t, d = x2d.shape
    width = wba.shape[0]
    kern = functools.partial(_mixer_kernel, tm=tm, tiles_per_seq=tiles_per_seq, width=width)
    row_spec = pl.BlockSpec((tm, d), lambda i: (i, 0))
    return pl.pallas_call(
        kern,
        grid=(t // tm,),
        in_specs=[row_spec, _full(gmix.shape), _full(win.shape), _full(wpool.shape),
                  _full(pscale.shape), _full(gsgu.shape), _full(ws.shape), _full(bsp.shape),
                  _full(convc.shape), _full(wba.shape), _full(wbb.shape), _full(wbc.shape),
                  _full(wo.shape)],
        out_specs=row_spec,
        out_shape=jax.ShapeDtypeStruct((t, d), jnp.float32),
        scratch_shapes=[
            pltpu.VMEM((tm, d), jnp.bfloat16),
            pltpu.VMEM((POOL_HALO + tm, width), jnp.float32),
            pltpu.VMEM((CONV_HALO + tm, width), jnp.float32),
            pltpu.VMEM((tm, width), jnp.bfloat16),
        ],
        compiler_params=pltpu.CompilerParams(
            dimension_semantics=("arbitrary",), vmem_limit_bytes=VMEM_LIMIT_BYTES),
        name="mixer",
    )(x2d, gmix, win, wpool, pscale, gsgu, ws, bsp, convc, wba, wbb, wbc, wo)


def _ffn_call(x2d, gffn, wup, convw, convb, wdown, gfin, *, tm, tiles_per_seq, final_norm):
    t, d = x2d.shape
    n_chunks = wdown.shape[0] // FF_CHUNK
    kern = functools.partial(_ffn_kernel, tm=tm, tiles_per_seq=tiles_per_seq,
                             final_norm=final_norm)
    row_spec = pl.BlockSpec((tm, d), lambda i: (i, 0))
    return pl.pallas_call(
        kern,
        grid=(t // tm,),
        in_specs=[row_spec, _full(gffn.shape), _full(wup.shape), _full(convw.shape),
                  _full(convb.shape), _full(wdown.shape), _full(gfin.shape)],
        out_specs=row_spec,
        out_shape=jax.ShapeDtypeStruct((t, d), jnp.float32),
        scratch_shapes=[
            pltpu.VMEM((tm, d), jnp.bfloat16),
            pltpu.VMEM((CONV_HALO + tm, 2 * FF_CHUNK), jnp.float32),
            pltpu.VMEM((n_chunks, CONV_HALO, 2 * FF_CHUNK), jnp.float32),
            pltpu.VMEM((tm, d), jnp.float32),
        ],
        compiler_params=pltpu.CompilerParams(
            dimension_semantics=("arbitrary",), vmem_limit_bytes=VMEM_LIMIT_BYTES),
        name="ffn",
    )(x2d, gffn, wup, convw, convb, wdown, gfin)


def _interleave_gate_value(w, d_ff):
    lead = w.shape[:-1]
    n_chunks = d_ff // FF_CHUNK
    w = w.reshape(lead + (2, n_chunks, FF_CHUNK))
    w = jnp.swapaxes(w, -3, -2)
    return w.reshape(lead + (2 * d_ff,))


def kernel(x, g_mix, w_in, w_pool, pool_scale, g_sgu, w_spatial, b_spatial, conv_c,
           w_branch_a, w_branch_b, w_branch_c, w_o, g_ffn, w_up, conv_ffn, conv_ffn_b,
           w_down, g_final):
    bsz, seq, d = x.shape
    depth = g_mix.shape[0]
    d_ff = w_down.shape[1]
    tm = TOKEN_TILE
    assert seq % tm == 0 and tm % CHUNK == 0 and d_ff % FF_CHUNK == 0
    assert w_spatial.shape[2] == CHUNK and w_pool.shape[1] == len(POOL_WINDOWS)
    tiles_per_seq = seq // tm
    bf16 = jnp.bfloat16

    h = x.reshape(bsz * seq, d)
    for l in range(depth):
        bsp = jnp.broadcast_to(b_spatial[l][:, :, None], (SGU_HEADS, CHUNK, LANES))
        h = _mixer_call(
            h, g_mix[l][None], w_in[l].astype(bf16), w_pool[l].astype(bf16), pool_scale[l][None],
            g_sgu[l][None], w_spatial[l], bsp, conv_c[l],
            w_branch_a[l].astype(bf16), w_branch_b[l].astype(bf16), w_branch_c[l].astype(bf16),
            w_o[l].astype(bf16), tm=tm, tiles_per_seq=tiles_per_seq)
        h = _ffn_call(
            h, g_ffn[l][None], _interleave_gate_value(w_up[l], d_ff).astype(bf16),
            _interleave_gate_value(conv_ffn[l], d_ff),
            _interleave_gate_value(conv_ffn_b[l], d_ff)[None],
            w_down[l].astype(bf16), g_final[None],
            tm=tm, tiles_per_seq=tiles_per_seq, final_norm=(l == depth - 1))
    return h.reshape(bsz, seq, d)
```

```python
import functools

import jax
import jax.numpy as jnp
from jax import lax
from jax.experimental import pallas as pl
from jax.experimental.pallas import tpu as pltpu

POOL_WINDOWS = (2, 4, 8, 16)
CHUNK = 128
SGU_HEADS = 4
CONV_WIDTH = 3
EPS = 1e-6

LANES = 128
SUBLANES = 8
POOL_HALO = 16
CONV_HALO = SUBLANES
FF_CHUNK = 256
UP_LOOKAHEAD = 11
UP_BUFFERS = UP_LOOKAHEAD + 1

TOKEN_TILE = 512
VMEM_LIMIT_BYTES = 56 * 1024 * 1024


def _dot(a, b):
    return jnp.dot(a, b, preferred_element_type=jnp.float32)


def _rmsnorm(x, g):
    ms = jnp.mean(x * x, axis=-1, keepdims=True)
    return x * lax.rsqrt(ms + EPS) * g


def _sigmoid(x):
    return 0.5 * jnp.tanh(0.5 * x) + 0.5


def _gelu_tanh(x):
    c = 0.7978845608028654
    return x * (0.5 * (1.0 + jnp.tanh(c * (x + 0.044715 * (x * x * x)))))


def _mixer_kernel(x_ref, gmix_ref, win_ref, wpool_ref, pscale_ref, gsgu_ref, ws_ref, bsp_ref,
                  convc_ref, wba_ref, wbb_ref, wbc_ref, wo_ref, out_ref,
                  hb_s, a_s, z_s, yb_s, *, tm, tiles_per_seq, width):
    f32, bf16 = jnp.float32, jnp.bfloat16
    wa = wb = wc = width
    d_model = x_ref.shape[1]
    seq_tile = lax.rem(pl.program_id(0), tiles_per_seq)

    @pl.when(seq_tile == 0)
    def _():
        a_s[0:POOL_HALO, :] = jnp.zeros((POOL_HALO, wa), f32)
        z_s[0:CONV_HALO, :] = jnp.zeros((CONV_HALO, wc), f32)

    x = x_ref[...]
    hb_s[...] = _rmsnorm(x, gmix_ref[...]).astype(bf16)

    o_a, o_u, o_v = 0, wa, wa + wb
    o_cb = wa + 2 * wb
    o_cc, o_cx = o_cb + wc, o_cb + 2 * wc
    o_ga = o_cb + 3 * wc
    o_gb, o_gc = o_ga + d_model, o_ga + 2 * d_model

    def project(offset, n):
        return _dot(hb_s[...], win_ref[:, offset:offset + n])

    a_s[POOL_HALO:POOL_HALO + tm, :] = project(o_a, wa)
    u = project(o_u, wb)
    v = project(o_v, wb)
    c_c = project(o_cc, wc)
    c_x = project(o_cx, wc)
    c_b = project(o_cb, wc)
    g_a = project(o_ga, d_model)
    g_b = project(o_gb, d_model)
    g_c = project(o_gc, d_model)

    pos = seq_tile * tm + lax.broadcasted_iota(jnp.int32, (tm, LANES), 0)
    ya_parts = []
    for g, w in enumerate(POOL_WINDOWS):
        lanes = slice(g * LANES, (g + 1) * LANES)
        cur = a_s[POOL_HALO:POOL_HALO + tm, lanes]
        s = cur
        for k in range(1, w):
            s = s + a_s[POOL_HALO - k:POOL_HALO - k + tm, lanes]
        cnt = jnp.minimum(pos + 1, w).astype(f32)
        pa = s / cnt - cur
        ya_parts.append(_dot(pa.astype(bf16), wpool_ref[g]))
    a_s[0:POOL_HALO, :] = a_s[tm:tm + POOL_HALO, :]
    ya = (jnp.concatenate(ya_parts, axis=1) * pscale_ref[...]).astype(bf16)

    u = _gelu_tanh(u)
    vn = _rmsnorm(_gelu_tanh(v), gsgu_ref[...]).astype(bf16)
    row = lax.broadcasted_iota(jnp.int32, (CHUNK, CHUNK), 0)
    col = lax.broadcasted_iota(jnp.int32, (CHUNK, CHUNK), 1)
    head_dim = wb // SGU_HEADS
    for g in range(SGU_HEADS):
        ws_g = jnp.where(row >= col, ws_ref[g], 0.0).astype(bf16)
        bias_g = bsp_ref[g]
        lanes = slice(g * head_dim, (g + 1) * head_dim)
        for c in range(tm // CHUNK):
            rows = slice(c * CHUNK, (c + 1) * CHUNK)
            sv = _dot(ws_g, vn[rows, lanes]) + bias_g
            yb_s[rows, lanes] = (u[rows, lanes] * sv).astype(bf16)

    z_s[CONV_HALO:CONV_HALO + tm, :] = c_c * c_x
    conv = z_s[CONV_HALO:CONV_HALO + tm, :] * convc_ref[CONV_WIDTH - 1:CONV_WIDTH, :]
    for k in range(1, CONV_WIDTH):
        conv = conv + (z_s[CONV_HALO - k:CONV_HALO - k + tm, :]
                       * convc_ref[CONV_WIDTH - 1 - k:CONV_WIDTH - k, :])
    z_s[0:CONV_HALO, :] = z_s[tm:tm + CONV_HALO, :]
    yc = (c_b * conv).astype(bf16)

    merged = (_sigmoid(g_a) * _dot(ya, wba_ref[...])
              + _sigmoid(g_b) * _dot(yb_s[...], wbb_ref[...])
              + _sigmoid(g_c) * _dot(yc, wbc_ref[...]))
    out_ref[...] = x_ref[...] + _dot(merged.astype(bf16), wo_ref[...])


def _ffn_kernel(x_ref, gffn_ref, wup_ref, convw_ref, convb_ref, wdown_ref, gfin_ref, out_ref,
                hb_s, up_s, carry_s, acc_s, *, tm, tiles_per_seq, final_norm):
    f32, bf16 = jnp.float32, jnp.bfloat16
    n_chunks = wdown_ref.shape[0] // FF_CHUNK
    seq_tile = lax.rem(pl.program_id(0), tiles_per_seq)

    @pl.when(seq_tile == 0)
    def _():
        carry_s[...] = jnp.zeros(carry_s.shape, f32)

    hb_s[...] = _rmsnorm(x_ref[...], gffn_ref[...]).astype(bf16)

    def chunk_cols(j):
        return slice(j * 2 * FF_CHUNK, (j + 1) * 2 * FF_CHUNK)

    def up_project(j):
        buf = up_s.at[j % UP_BUFFERS]
        buf[0:CONV_HALO, :] = carry_s[j]
        buf[CONV_HALO:CONV_HALO + tm, :] = _dot(hb_s[...], wup_ref[:, chunk_cols(j)])
        carry_s[j] = buf[tm:tm + CONV_HALO, :]

    for j in range(min(UP_LOOKAHEAD, n_chunks)):
        up_project(j)
    for j in range(n_chunks):
        if j + UP_LOOKAHEAD < n_chunks:
            up_project(j + UP_LOOKAHEAD)
        cols = chunk_cols(j)
        buf = up_s.at[j % UP_BUFFERS]
        conv = (buf[CONV_HALO:CONV_HALO + tm, :] * convw_ref[CONV_WIDTH - 1:CONV_WIDTH, cols]
                + convb_ref[:, cols])
        for k in range(1, CONV_WIDTH):
            conv = conv + (buf[CONV_HALO - k:CONV_HALO - k + tm, :]
                           * convw_ref[CONV_WIDTH - 1 - k:CONV_WIDTH - k, cols])
        half_gate = 0.5 * conv[:, :FF_CHUNK]
        act = (half_gate * (1.0 + jnp.tanh(half_gate)) * conv[:, FF_CHUNK:]).astype(bf16)
        down = _dot(act, wdown_ref[j * FF_CHUNK:(j + 1) * FF_CHUNK, :])
        if j == 0:
            acc_s[...] = x_ref[...] + down
        else:
            acc_s[...] += down

    y = acc_s[...]
    if final_norm:
        y = _rmsnorm(y, gfin_ref[...])
    out_ref[...] = y


def _full(shape):
    return pl.BlockSpec(shape, lambda i: (0,) * len(shape))


def _mixer_call(x2d, gmix, win, wpool, pscale, gsgu, ws, bsp, convc, wba, wbb, wbc, wo,
                *, tm, tiles_per_seq):
    t, d = x2d.shape
    width = wba.shape[0]
    kern = functools.partial(_mixer_kernel, tm=tm, tiles_per_seq=tiles_per_seq, width=width)
    row_spec = pl.BlockSpec((tm, d), lambda i: (i, 0))
    return pl.pallas_call(
        kern,
        grid=(t // tm,),
        in_specs=[row_spec, _full(gmix.shape), _full(win.shape), _full(wpool.shape),
                  _full(pscale.shape), _full(gsgu.shape), _full(ws.shape), _full(bsp.shape),
                  _full(convc.shape), _full(wba.shape), _full(wbb.shape), _full(wbc.shape),
                  _full(wo.shape)],
        out_specs=row_spec,
        out_shape=jax.ShapeDtypeStruct((t, d), jnp.float32),
        scratch_shapes=[
            pltpu.VMEM((tm, d), jnp.bfloat16),
            pltpu.VMEM((POOL_HALO + tm, width), jnp.float32),
            pltpu.VMEM((CONV_HALO + tm, width), jnp.float32),
            pltpu.VMEM((tm, width), jnp.bfloat16),
        ],
        compiler_params=pltpu.CompilerParams(
            dimension_semantics=("arbitrary",), vmem_limit_bytes=VMEM_LIMIT_BYTES),
        name="mixer",
    )(x2d, gmix, win, wpool, pscale, gsgu, ws, bsp, convc, wba, wbb, wbc, wo)


def _ffn_call(x2d, gffn, wup, convw, convb, wdown, gfin, *, tm, tiles_per_seq, final_norm):
    t, d = x2d.shape
    n_chunks = wdown.shape[0] // FF_CHUNK
    kern = functools.partial(_ffn_kernel, tm=tm, tiles_per_seq=tiles_per_seq,
                             final_norm=final_norm)
    row_spec = pl.BlockSpec((tm, d), lambda i: (i, 0))
    return pl.pallas_call(
        kern,
        grid=(t // tm,),
        in_specs=[row_spec, _full(gffn.shape), _full(wup.shape), _full(convw.shape),
                  _full(convb.shape), _full(wdown.shape), _full(gfin.shape)],
        out_specs=row_spec,
        out_shape=jax.ShapeDtypeStruct((t, d), jnp.float32),
        scratch_shapes=[
            pltpu.VMEM((tm, d), jnp.bfloat16),
            pltpu.VMEM((UP_BUFFERS, CONV_HALO + tm, 2 * FF_CHUNK), jnp.float32),
            pltpu.VMEM((n_chunks, CONV_HALO, 2 * FF_CHUNK), jnp.float32),
            pltpu.VMEM((tm, d), jnp.float32),
        ],
        compiler_params=pltpu.CompilerParams(
            dimension_semantics=("arbitrary",), vmem_limit_bytes=VMEM_LIMIT_BYTES),
        name="ffn",
    )(x2d, gffn, wup, convw, convb, wdown, gfin)


def _interleave_gate_value(w, d_ff):
    lead = w.shape[:-1]
    n_chunks = d_ff // FF_CHUNK
    w = w.reshape(lead + (2, n_chunks, FF_CHUNK))
    w = jnp.swapaxes(w, -3, -2)
    return w.reshape(lead + (2 * d_ff,))


def kernel(x, g_mix, w_in, w_pool, pool_scale, g_sgu, w_spatial, b_spatial, conv_c,
           w_branch_a, w_branch_b, w_branch_c, w_o, g_ffn, w_up, conv_ffn, conv_ffn_b,
           w_down, g_final):
    bsz, seq, d = x.shape
    depth = g_mix.shape[0]
    d_ff = w_down.shape[1]
    tm = TOKEN_TILE
    assert seq % tm == 0 and tm % CHUNK == 0 and d_ff % FF_CHUNK == 0
    assert w_spatial.shape[2] == CHUNK and w_pool.shape[1] == len(POOL_WINDOWS)
    tiles_per_seq = seq // tm
    bf16 = jnp.bfloat16

    h = x.reshape(bsz * seq, d)
    for l in range(depth):
        bsp = jnp.broadcast_to(b_spatial[l][:, :, None], (SGU_HEADS, CHUNK, LANES))
        h = _mixer_call(
            h, g_mix[l][None], w_in[l].astype(bf16), w_pool[l].astype(bf16), pool_scale[l][None],
            g_sgu[l][None], w_spatial[l], bsp, conv_c[l],
            w_branch_a[l].astype(bf16), w_branch_b[l].astype(bf16), w_branch_c[l].astype(bf16),
            w_o[l].astype(bf16), tm=tm, tiles_per_seq=tiles_per_seq)
        h = _ffn_call(
            h, g_ffn[l][None], _interleave_gate_value(w_up[l], d_ff).astype(bf16),
            _interleave_gate_value(conv_ffn[l], d_ff),
            _interleave_gate_value(conv_ffn_b[l], d_ff)[None],
            w_down[l].astype(bf16), g_final[None],
            tm=tm, tiles_per_seq=tiles_per_seq, final_norm=(l == depth - 1))
    return h.reshape(bsz, seq, d)
```

```python
import functools

import jax
import jax.numpy as jnp
from jax import lax
from jax.experimental import pallas as pl
from jax.experimental.pallas import tpu as pltpu

POOL_WINDOWS = (2, 4, 8, 16)
CHUNK = 128
SGU_HEADS = 4
CONV_WIDTH = 3
EPS = 1e-6

LANES = 128
SUBLANES = 8
POOL_HALO = 16
CONV_HALO = SUBLANES
FF_CHUNK = 512

TOKEN_TILE = 512
VMEM_LIMIT_BYTES = 56 * 1024 * 1024


def _dot(a, b):
    return jnp.dot(a, b, preferred_element_type=jnp.float32)


def _store_lane_tiles(ref, row0, value):
    rows = value.shape[0]
    for t in range(value.shape[1] // LANES):
        ref[t, row0:row0 + rows, :] = value[:, t * LANES:(t + 1) * LANES]


def _rmsnorm(x, g):
    ms = jnp.mean(x * x, axis=-1, keepdims=True)
    return x * lax.rsqrt(ms + EPS) * g


def _sigmoid(x):
    return 0.5 * jnp.tanh(0.5 * x) + 0.5


def _gelu_tanh(x):
    c = 0.7978845608028654
    return x * (0.5 * (1.0 + jnp.tanh(c * (x + 0.044715 * (x * x * x)))))


def _mixer_kernel(x_ref, gmix_ref, win_ref, wpool_ref, pscale_ref, gsgu_ref, ws_ref, bsp_ref,
                  convc_ref, wba_ref, wbb_ref, wbc_ref, wo_ref, out_ref,
                  hb_s, a_s, z_s, yb_s, *, tm, tiles_per_seq, width):
    f32, bf16 = jnp.float32, jnp.bfloat16
    wa = wb = wc = width
    d_model = x_ref.shape[1]
    seq_tile = lax.rem(pl.program_id(0), tiles_per_seq)

    @pl.when(seq_tile == 0)
    def _():
        a_s[:, 0:POOL_HALO, :] = jnp.zeros((wa // LANES, POOL_HALO, LANES), f32)
        z_s[:, 0:CONV_HALO, :] = jnp.zeros((wc // LANES, CONV_HALO, LANES), f32)

    hb_s[...] = _rmsnorm(x_ref[...], gmix_ref[...]).astype(bf16)

    o_a, o_u, o_v = 0, wa, wa + wb
    o_cb = wa + 2 * wb
    o_cc, o_cx = o_cb + wc, o_cb + 2 * wc
    o_ga = o_cb + 3 * wc
    o_gb, o_gc = o_ga + d_model, o_ga + 2 * d_model

    def project(offset, n):
        return _dot(hb_s[...], win_ref[:, offset:offset + n])

    _store_lane_tiles(a_s, POOL_HALO, project(o_a, wa))
    u = project(o_u, wb)
    v = project(o_v, wb)
    c_c = project(o_cc, wc)
    c_x = project(o_cx, wc)
    c_b = project(o_cb, wc)
    g_a = project(o_ga, d_model)
    g_b = project(o_gb, d_model)
    g_c = project(o_gc, d_model)

    pos = seq_tile * tm + lax.broadcasted_iota(jnp.int32, (tm, LANES), 0)
    ya_parts = []
    for g, w in enumerate(POOL_WINDOWS):
        cur = a_s[g, POOL_HALO:POOL_HALO + tm, :]
        s = cur
        for k in range(1, w):
            s = s + a_s[g, POOL_HALO - k:POOL_HALO - k + tm, :]
        cnt = jnp.minimum(pos + 1, w).astype(f32)
        pa = s / cnt - cur
        ya_parts.append(_dot(pa.astype(bf16), wpool_ref[g]))
    a_s[:, 0:POOL_HALO, :] = a_s[:, tm:tm + POOL_HALO, :]
    ya = (jnp.concatenate(ya_parts, axis=1) * pscale_ref[...]).astype(bf16)

    u = _gelu_tanh(u)
    vn = _rmsnorm(_gelu_tanh(v), gsgu_ref[...]).astype(bf16)
    row = lax.broadcasted_iota(jnp.int32, (CHUNK, CHUNK), 0)
    col = lax.broadcasted_iota(jnp.int32, (CHUNK, CHUNK), 1)
    head_dim = wb // SGU_HEADS
    for g in range(SGU_HEADS):
        ws_g = jnp.where(row >= col, ws_ref[g], 0.0).astype(bf16)
        bias_g = bsp_ref[g]
        lanes = slice(g * head_dim, (g + 1) * head_dim)
        for c in range(tm // CHUNK):
            rows = slice(c * CHUNK, (c + 1) * CHUNK)
            sv = _dot(ws_g, vn[rows, lanes]) + bias_g
            yb_s[rows, lanes] = (u[rows, lanes] * sv).astype(bf16)

    _store_lane_tiles(z_s, CONV_HALO, c_c * c_x)
    conv_parts = []
    for t in range(wc // LANES):
        lc = slice(t * LANES, (t + 1) * LANES)
        conv = z_s[t, CONV_HALO:CONV_HALO + tm, :] * convc_ref[CONV_WIDTH - 1:CONV_WIDTH, lc]
        for k in range(1, CONV_WIDTH):
            conv = conv + (z_s[t, CONV_HALO - k:CONV_HALO - k + tm, :]
                           * convc_ref[CONV_WIDTH - 1 - k:CONV_WIDTH - k, lc])
        conv_parts.append(conv)
    z_s[:, 0:CONV_HALO, :] = z_s[:, tm:tm + CONV_HALO, :]
    yc = (c_b * jnp.concatenate(conv_parts, axis=1)).astype(bf16)

    merged = (_sigmoid(g_a) * _dot(ya, wba_ref[...])
              + _sigmoid(g_b) * _dot(yb_s[...], wbb_ref[...])
              + _sigmoid(g_c) * _dot(yc, wbc_ref[...]))
    out_ref[...] = x_ref[...] + _dot(merged.astype(bf16), wo_ref[...])


def _ffn_kernel(x_ref, gffn_ref, wup_ref, convw_ref, convb_ref, wdown_ref, gfin_ref, out_ref,
                hb_s, up_s, act_s, acc_s, *, tm, tiles_per_seq, final_norm):
    f32, bf16 = jnp.float32, jnp.bfloat16
    d_ff = wdown_ref.shape[0]
    seq_tile = lax.rem(pl.program_id(0), tiles_per_seq)

    @pl.when(seq_tile == 0)
    def _():
        up_s[:, 0:CONV_HALO, :] = jnp.zeros((2 * d_ff // LANES, CONV_HALO, LANES), f32)

    hb_s[...] = _rmsnorm(x_ref[...], gffn_ref[...]).astype(bf16)

    chunks = [(lo, min(lo + FF_CHUNK, d_ff)) for lo in range(0, d_ff, FF_CHUNK)]

    def up_project(j):
        lo, hi = chunks[j]
        for base in (lo, d_ff + lo):
            tiles = up_s.at[base // LANES:(base + hi - lo) // LANES]
            _store_lane_tiles(tiles, CONV_HALO, _dot(hb_s[...], wup_ref[:, base:base + hi - lo]))

    def conv3(cols):
        parts = []
        for t in range(cols.start // LANES, cols.stop // LANES):
            lc = slice(t * LANES, (t + 1) * LANES)
            out = (up_s[t, CONV_HALO:CONV_HALO + tm, :] * convw_ref[CONV_WIDTH - 1:CONV_WIDTH, lc]
                   + convb_ref[:, lc])
            for k in range(1, CONV_WIDTH):
                out = out + (up_s[t, CONV_HALO - k:CONV_HALO - k + tm, :]
                             * convw_ref[CONV_WIDTH - 1 - k:CONV_WIDTH - k, lc])
            parts.append(out)
        return jnp.concatenate(parts, axis=1)

    def activate(j):
        lo, hi = chunks[j]
        half_gate = 0.5 * conv3(slice(lo, hi))
        value = conv3(slice(d_ff + lo, d_ff + hi))
        act_s[:, lo:hi] = (half_gate * (1.0 + jnp.tanh(half_gate)) * value).astype(bf16)

    def down_project(j):
        lo, hi = chunks[j]
        down = _dot(act_s[:, lo:hi], wdown_ref[lo:hi, :])
        if j == 0:
            acc_s[...] = x_ref[...] + down
        else:
            acc_s[...] += down

    for j in range(len(chunks)):
        up_project(j)
    for j in range(len(chunks)):
        activate(j)
        down_project(j)
    up_s[:, 0:CONV_HALO, :] = up_s[:, tm:tm + CONV_HALO, :]

    y = acc_s[...]
    if final_norm:
        y = _rmsnorm(y, gfin_ref[...])
    out_ref[...] = y


def _full(shape):
    return pl.BlockSpec(shape, lambda i: (0,) * len(shape))


def _mixer_call(x2d, gmix, win, wpool, pscale, gsgu, ws, bsp, convc, wba, wbb, wbc, wo,
                *, tm, tiles_per_seq):
    t, d = x2d.shape
    width = wba.shape[0]
    kern = functools.partial(_mixer_kernel, tm=tm, tiles_per_seq=tiles_per_seq, width=width)
    row_spec = pl.BlockSpec((tm, d), lambda i: (i, 0))
    return pl.pallas_call(
        kern,
        grid=(t // tm,),
        in_specs=[row_spec, _full(gmix.shape), _full(win.shape), _full(wpool.shape),
                  _full(pscale.shape), _full(gsgu.shape), _full(ws.shape), _full(bsp.shape),
                  _full(convc.shape), _full(wba.shape), _full(wbb.shape), _full(wbc.shape),
                  _full(wo.shape)],
        out_specs=row_spec,
        out_shape=jax.ShapeDtypeStruct((t, d), jnp.float32),
        scratch_shapes=[
            pltpu.VMEM((tm, d), jnp.bfloat16),
            pltpu.VMEM((width // LANES, POOL_HALO + tm, LANES), jnp.float32),
            pltpu.VMEM((width // LANES, CONV_HALO + tm, LANES), jnp.float32),
            pltpu.VMEM((tm, width), jnp.bfloat16),
        ],
        compiler_params=pltpu.CompilerParams(
            dimension_semantics=("arbitrary",), vmem_limit_bytes=VMEM_LIMIT_BYTES),
        name="mixer",
    )(x2d, gmix, win, wpool, pscale, gsgu, ws, bsp, convc, wba, wbb, wbc, wo)


def _ffn_call(x2d, gffn, wup, convw, convb, wdown, gfin, *, tm, tiles_per_seq, final_norm):
    t, d = x2d.shape
    d_ff = wdown.shape[0]
    kern = functools.partial(_ffn_kernel, tm=tm, tiles_per_seq=tiles_per_seq,
                             final_norm=final_norm)
    row_spec = pl.BlockSpec((tm, d), lambda i: (i, 0))
    return pl.pallas_call(
        kern,
        grid=(t // tm,),
        in_specs=[row_spec, _full(gffn.shape), _full(wup.shape), _full(convw.shape),
                  _full(convb.shape), _full(wdown.shape), _full(gfin.shape)],
        out_specs=row_spec,
        out_shape=jax.ShapeDtypeStruct((t, d), jnp.float32),
        scratch_shapes=[
            pltpu.VMEM((tm, d), jnp.bfloat16),
            pltpu.VMEM((2 * d_ff // LANES, CONV_HALO + tm, LANES), jnp.float32),
            pltpu.VMEM((tm, d_ff), jnp.bfloat16),
            pltpu.VMEM((tm, d), jnp.float32),
        ],
        compiler_params=pltpu.CompilerParams(
            dimension_semantics=("arbitrary",), vmem_limit_bytes=VMEM_LIMIT_BYTES),
        name="ffn",
    )(x2d, gffn, wup, convw, convb, wdown, gfin)


def kernel(x, g_mix, w_in, w_pool, pool_scale, g_sgu, w_spatial, b_spatial, conv_c,
           w_branch_a, w_branch_b, w_branch_c, w_o, g_ffn, w_up, conv_ffn, conv_ffn_b,
           w_down, g_final):
    bsz, seq, d = x.shape
    depth = g_mix.shape[0]
    d_ff = w_down.shape[1]
    tm = TOKEN_TILE
    assert seq % tm == 0 and tm % CHUNK == 0 and d_ff % LANES == 0
    assert w_spatial.shape[2] == CHUNK and w_pool.shape[1] == len(POOL_WINDOWS)
    tiles_per_seq = seq // tm
    bf16 = jnp.bfloat16

    h = x.reshape(bsz * seq, d)
    for l in range(depth):
        bsp = jnp.broadcast_to(b_spatial[l][:, :, None], (SGU_HEADS, CHUNK, LANES))
        h = _mixer_call(
            h, g_mix[l][None], w_in[l].astype(bf16), w_pool[l].astype(bf16), pool_scale[l][None],
            g_sgu[l][None], w_spatial[l], bsp, conv_c[l],
            w_branch_a[l].astype(bf16), w_branch_b[l].astype(bf16), w_branch_c[l].astype(bf16),
            w_o[l].astype(bf16), tm=tm, tiles_per_seq=tiles_per_seq)
        h = _ffn_call(
            h, g_ffn[l][None], w_up[l].astype(bf16), conv_ffn[l], conv_ffn_b[l][None],
            w_down[l].astype(bf16), g_final[None],
            tm=tm, tiles_per_seq=tiles_per_seq, final_norm=(l == depth - 1))
    return h.reshape(bsz, seq, d)
```

```python
import functools

import jax
import jax.numpy as jnp
from jax import lax
from jax.experimental import pallas as pl
from jax.experimental.pallas import tpu as pltpu

POOL_WINDOWS = (2, 4, 8, 16)
CHUNK = 128
SGU_HEADS = 4
CONV_WIDTH = 3
EPS = 1e-6

LANES = 128
SUBLANES = 8
POOL_HALO = 16
CONV_HALO = SUBLANES
FF_CHUNK = 512

TOKEN_TILE = 512
VMEM_LIMIT_BYTES = 56 * 1024 * 1024


def _dot(a, b):
    return jnp.dot(a, b, preferred_element_type=jnp.float32)


def _store_lane_tiles(ref, row0, value):
    rows = value.shape[0]
    for t in range(value.shape[1] // LANES):
        ref[t, row0:row0 + rows, :] = value[:, t * LANES:(t + 1) * LANES]


def _rmsnorm(x, g):
    ms = jnp.mean(x * x, axis=-1, keepdims=True)
    return x * lax.rsqrt(ms + EPS) * g


def _sigmoid(x):
    return 0.5 * jnp.tanh(0.5 * x) + 0.5


def _gelu_tanh(x):
    c = 0.7978845608028654
    return x * (0.5 * (1.0 + jnp.tanh(c * (x + 0.044715 * (x * x * x)))))


def _mixer_kernel(x_ref, gmix_ref, win_ref, wpool_ref, pscale_ref, gsgu_ref, ws_ref, bsp_ref,
                  convc_ref, wba_ref, wbb_ref, wbc_ref, wo_ref, out_ref,
                  hb_s, a_s, z_s, yb_s, *, tm, tiles_per_seq, width):
    f32, bf16 = jnp.float32, jnp.bfloat16
    wa = wb = wc = width
    d_model = x_ref.shape[1]
    seq_tile = lax.rem(pl.program_id(0), tiles_per_seq)

    @pl.when(seq_tile == 0)
    def _():
        a_s[:, 0:POOL_HALO, :] = jnp.zeros((wa // LANES, POOL_HALO, LANES), f32)
        z_s[:, 0:CONV_HALO, :] = jnp.zeros((wc // LANES, CONV_HALO, LANES), f32)

    hb_s[...] = _rmsnorm(x_ref[...], gmix_ref[...]).astype(bf16)

    o_a, o_u, o_v = 0, wa, wa + wb
    o_cb = wa + 2 * wb
    o_cc, o_cx = o_cb + wc, o_cb + 2 * wc
    o_ga = o_cb + 3 * wc
    o_gb, o_gc = o_ga + d_model, o_ga + 2 * d_model

    def project(offset, n):
        return _dot(hb_s[...], win_ref[:, offset:offset + n])

    _store_lane_tiles(a_s, POOL_HALO, project(o_a, wa))
    u = project(o_u, wb)
    v = project(o_v, wb)
    c_c = project(o_cc, wc)
    c_x = project(o_cx, wc)
    c_b = project(o_cb, wc)
    g_a = project(o_ga, d_model)
    g_b = project(o_gb, d_model)
    g_c = project(o_gc, d_model)

    pos = seq_tile * tm + lax.broadcasted_iota(jnp.int32, (tm, LANES), 0)
    ya_parts = []
    for g, w in enumerate(POOL_WINDOWS):
        cur = a_s[g, POOL_HALO:POOL_HALO + tm, :]
        s = cur
        for k in range(1, w):
            s = s + a_s[g, POOL_HALO - k:POOL_HALO - k + tm, :]
        cnt = jnp.minimum(pos + 1, w).astype(f32)
        pa = s / cnt - cur
        ya_parts.append(_dot(pa.astype(bf16), wpool_ref[g]))
    a_s[:, 0:POOL_HALO, :] = a_s[:, tm:tm + POOL_HALO, :]
    ya = (jnp.concatenate(ya_parts, axis=1) * pscale_ref[...]).astype(bf16)

    u = _gelu_tanh(u)
    vn = _rmsnorm(_gelu_tanh(v), gsgu_ref[...]).astype(bf16)
    row = lax.broadcasted_iota(jnp.int32, (CHUNK, CHUNK), 0)
    col = lax.broadcasted_iota(jnp.int32, (CHUNK, CHUNK), 1)
    head_dim = wb // SGU_HEADS
    for g in range(SGU_HEADS):
        ws_g = jnp.where(row >= col, ws_ref[g], 0.0).astype(bf16)
        bias_g = bsp_ref[g]
        lanes = slice(g * head_dim, (g + 1) * head_dim)
        for c in range(tm // CHUNK):
            rows = slice(c * CHUNK, (c + 1) * CHUNK)
            sv = _dot(ws_g, vn[rows, lanes]) + bias_g
            yb_s[rows, lanes] = (u[rows, lanes] * sv).astype(bf16)

    _store_lane_tiles(z_s, CONV_HALO, c_c * c_x)
    conv_parts = []
    for t in range(wc // LANES):
        lc = slice(t * LANES, (t + 1) * LANES)
        conv = z_s[t, CONV_HALO:CONV_HALO + tm, :] * convc_ref[CONV_WIDTH - 1:CONV_WIDTH, lc]
        for k in range(1, CONV_WIDTH):
            conv = conv + (z_s[t, CONV_HALO - k:CONV_HALO - k + tm, :]
                           * convc_ref[CONV_WIDTH - 1 - k:CONV_WIDTH - k, lc])
        conv_parts.append(conv)
    z_s[:, 0:CONV_HALO, :] = z_s[:, tm:tm + CONV_HALO, :]
    yc = (c_b * jnp.concatenate(conv_parts, axis=1)).astype(bf16)

    merged = (_sigmoid(g_a) * _dot(ya, wba_ref[...])
              + _sigmoid(g_b) * _dot(yb_s[...], wbb_ref[...])
              + _sigmoid(g_c) * _dot(yc, wbc_ref[...]))
    out_ref[...] = x_ref[...] + _dot(merged.astype(bf16), wo_ref[...])


def _ffn_kernel(x_ref, gffn_ref, wup_ref, convw_ref, convb_ref, wdown_ref, gfin_ref, out_ref,
                hb_s, up_s, act_s, acc_s, *, tm, tiles_per_seq, final_norm):
    f32, bf16 = jnp.float32, jnp.bfloat16
    d_ff = wdown_ref.shape[0]
    seq_tile = lax.rem(pl.program_id(0), tiles_per_seq)

    @pl.when(seq_tile == 0)
    def _():
        up_s[:, 0:CONV_HALO, :] = jnp.zeros((2 * d_ff // LANES, CONV_HALO, LANES), f32)

    hb_s[...] = _rmsnorm(x_ref[...], gffn_ref[...]).astype(bf16)

    chunks = [(lo, min(lo + FF_CHUNK, d_ff)) for lo in range(0, d_ff, FF_CHUNK)]

    def up_project(j):
        lo, hi = chunks[j]
        for base in (lo, d_ff + lo):
            tiles = up_s.at[base // LANES:(base + hi - lo) // LANES]
            _store_lane_tiles(tiles, CONV_HALO, _dot(hb_s[...], wup_ref[:, base:base + hi - lo]))

    def conv3(cols):
        parts = []
        for t in range(cols.start // LANES, cols.stop // LANES):
            lc = slice(t * LANES, (t + 1) * LANES)
            out = (up_s[t, CONV_HALO:CONV_HALO + tm, :] * convw_ref[CONV_WIDTH - 1:CONV_WIDTH, lc]
                   + convb_ref[:, lc])
            for k in range(1, CONV_WIDTH):
                out = out + (up_s[t, CONV_HALO - k:CONV_HALO - k + tm, :]
                             * convw_ref[CONV_WIDTH - 1 - k:CONV_WIDTH - k, lc])
            parts.append(out)
        return jnp.concatenate(parts, axis=1)

    def activate(j):
        lo, hi = chunks[j]
        half_gate = 0.5 * conv3(slice(lo, hi))
        value = conv3(slice(d_ff + lo, d_ff + hi))
        act_s[:, lo:hi] = (half_gate * (1.0 + jnp.tanh(half_gate)) * value).astype(bf16)

    def down_project(j):
        lo, hi = chunks[j]
        down = _dot(act_s[:, lo:hi], wdown_ref[lo:hi, :])
        if j == 0:
            acc_s[...] = x_ref[...] + down
        else:
            acc_s[...] += down

    for j in range(len(chunks)):
        up_project(j)
    for j in range(len(chunks)):
        activate(j)
        down_project(j)
    up_s[:, 0:CONV_HALO, :] = up_s[:, tm:tm + CONV_HALO, :]

    y = acc_s[...]
    if final_norm:
        y = _rmsnorm(y, gfin_ref[...])
    out_ref[...] = y


def _full(shape):
    return pl.BlockSpec(shape, lambda i: (0,) * len(shape))


def _layer(shape, layer):
    return pl.BlockSpec((None,) + tuple(shape[1:]), lambda i: (layer,) + (0,) * (len(shape) - 1),
                        pipeline_mode=pl.Buffered(1))


def _mixer_call(x2d, gmix, win, wpool, pscale, gsgu, ws, bsp, convc, wba, wbb, wbc, wo,
                *, layer, tm, tiles_per_seq):
    t, d = x2d.shape
    width = wba.shape[1]
    kern = functools.partial(_mixer_kernel, tm=tm, tiles_per_seq=tiles_per_seq, width=width)
    row_spec = pl.BlockSpec((tm, d), lambda i: (i, 0))
    return pl.pallas_call(
        kern,
        grid=(t // tm,),
        in_specs=[row_spec, _full(gmix.shape), _layer(win.shape, layer), _full(wpool.shape),
                  _full(pscale.shape), _full(gsgu.shape), _full(ws.shape), _full(bsp.shape),
                  _full(convc.shape), _layer(wba.shape, layer), _layer(wbb.shape, layer),
                  _layer(wbc.shape, layer), _layer(wo.shape, layer)],
        out_specs=row_spec,
        out_shape=jax.ShapeDtypeStruct((t, d), jnp.float32),
        scratch_shapes=[
            pltpu.VMEM((tm, d), jnp.bfloat16),
            pltpu.VMEM((width // LANES, POOL_HALO + tm, LANES), jnp.float32),
            pltpu.VMEM((width // LANES, CONV_HALO + tm, LANES), jnp.float32),
            pltpu.VMEM((tm, width), jnp.bfloat16),
        ],
        compiler_params=pltpu.CompilerParams(
            dimension_semantics=("arbitrary",), vmem_limit_bytes=VMEM_LIMIT_BYTES),
        name="mixer",
    )(x2d, gmix, win, wpool, pscale, gsgu, ws, bsp, convc, wba, wbb, wbc, wo)


def _ffn_call(x2d, gffn, wup, convw, convb, wdown, gfin,
              *, layer, tm, tiles_per_seq, final_norm):
    t, d = x2d.shape
    d_ff = wdown.shape[1]
    kern = functools.partial(_ffn_kernel, tm=tm, tiles_per_seq=tiles_per_seq,
                             final_norm=final_norm)
    row_spec = pl.BlockSpec((tm, d), lambda i: (i, 0))
    return pl.pallas_call(
        kern,
        grid=(t // tm,),
        in_specs=[row_spec, _full(gffn.shape), _layer(wup.shape, layer), _full(convw.shape),
                  _full(convb.shape), _layer(wdown.shape, layer), _full(gfin.shape)],
        out_specs=row_spec,
        out_shape=jax.ShapeDtypeStruct((t, d), jnp.float32),
        scratch_shapes=[
            pltpu.VMEM((tm, d), jnp.bfloat16),
            pltpu.VMEM((2 * d_ff // LANES, CONV_HALO + tm, LANES), jnp.float32),
            pltpu.VMEM((tm, d_ff), jnp.bfloat16),
            pltpu.VMEM((tm, d), jnp.float32),
        ],
        compiler_params=pltpu.CompilerParams(
            dimension_semantics=("arbitrary",), vmem_limit_bytes=VMEM_LIMIT_BYTES),
        name="ffn",
    )(x2d, gffn, wup, convw, convb, wdown, gfin)


def kernel(x, g_mix, w_in, w_pool, pool_scale, g_sgu, w_spatial, b_spatial, conv_c,
           w_branch_a, w_branch_b, w_branch_c, w_o, g_ffn, w_up, conv_ffn, conv_ffn_b,
           w_down, g_final):
    bsz, seq, d = x.shape
    depth = g_mix.shape[0]
    d_ff = w_down.shape[1]
    tm = TOKEN_TILE
    assert seq % tm == 0 and tm % CHUNK == 0 and d_ff % LANES == 0
    assert w_spatial.shape[2] == CHUNK and w_pool.shape[1] == len(POOL_WINDOWS)
    tiles_per_seq = seq // tm
    bf16 = jnp.bfloat16

    win, wba, wbb, wbc, wo, wup, wdown = (
        w.astype(bf16) for w in (w_in, w_branch_a, w_branch_b, w_branch_c, w_o, w_up, w_down))

    h = x.reshape(bsz * seq, d)
    for l in range(depth):
        bsp = jnp.broadcast_to(b_spatial[l][:, :, None], (SGU_HEADS, CHUNK, LANES))
        h = _mixer_call(
            h, g_mix[l][None], win, w_pool[l].astype(bf16), pool_scale[l][None],
            g_sgu[l][None], w_spatial[l], bsp, conv_c[l], wba, wbb, wbc, wo,
            layer=l, tm=tm, tiles_per_seq=tiles_per_seq)
        h = _ffn_call(
            h, g_ffn[l][None], wup, conv_ffn[l], conv_ffn_b[l][None], wdown, g_final[None],
            layer=l, tm=tm, tiles_per_seq=tiles_per_seq, final_norm=(l == depth - 1))
    return h.reshape(bsz, seq, d)
```

```python
import functools

import jax
import jax.numpy as jnp
from jax import lax
from jax.experimental import pallas as pl
from jax.experimental.pallas import tpu as pltpu

POOL_WINDOWS = (2, 4, 8, 16)
CHUNK = 128
SGU_HEADS = 4
CONV_WIDTH = 3
EPS = 1e-6

LANES = 128
SUBLANES = 8
POOL_HALO = 16
CONV_HALO = SUBLANES
FF_CHUNK = 512

V7X_VMEM_BYTES = 64 * 1024 * 1024
VMEM_LIMIT_BYTES = V7X_VMEM_BYTES - 2 * 1024 * 1024

MIXER_TILE = 1024
FFN_TILE = 512


def _dot(a, b):
    return jnp.dot(a, b, preferred_element_type=jnp.float32)


def _store_lane_tiles(ref, row0, value):
    rows = value.shape[0]
    for t in range(value.shape[1] // LANES):
        ref[t, row0:row0 + rows, :] = value[:, t * LANES:(t + 1) * LANES]


def _rmsnorm(x, g):
    ms = jnp.mean(x * x, axis=-1, keepdims=True)
    return x * lax.rsqrt(ms + EPS) * g


def _sigmoid(x):
    return 0.5 * jnp.tanh(0.5 * x) + 0.5


def _gelu_tanh(x):
    c = 0.7978845608028654
    return x * (0.5 * (1.0 + jnp.tanh(c * (x + 0.044715 * (x * x * x)))))


def _mixer_kernel(x_ref, gmix_ref, win_ref, wpool_ref, pscale_ref, gsgu_ref, ws_ref, bsp_ref,
                  convc_ref, wba_ref, wbb_ref, wbc_ref, wo_ref, out_ref,
                  hb_s, a_s, z_s, yb_s, *, tm, tiles_per_seq, width):
    f32, bf16 = jnp.float32, jnp.bfloat16
    wa = wb = wc = width
    d_model = x_ref.shape[1]
    seq_tile = lax.rem(pl.program_id(0), tiles_per_seq)

    @pl.when(seq_tile == 0)
    def _():
        a_s[:, 0:POOL_HALO, :] = jnp.zeros((wa // LANES, POOL_HALO, LANES), f32)
        z_s[:, 0:CONV_HALO, :] = jnp.zeros((wc // LANES, CONV_HALO, LANES), f32)

    hb_s[...] = _rmsnorm(x_ref[...], gmix_ref[...]).astype(bf16)

    o_a, o_u, o_v = 0, wa, wa + wb
    o_cb = wa + 2 * wb
    o_cc, o_cx = o_cb + wc, o_cb + 2 * wc
    o_ga = o_cb + 3 * wc
    o_gb, o_gc = o_ga + d_model, o_ga + 2 * d_model

    def project(offset, n):
        return _dot(hb_s[...], win_ref[:, offset:offset + n])

    _store_lane_tiles(a_s, POOL_HALO, project(o_a, wa))
    u = project(o_u, wb)
    v = project(o_v, wb)
    c_c = project(o_cc, wc)
    c_x = project(o_cx, wc)
    c_b = project(o_cb, wc)
    g_a = project(o_ga, d_model)
    g_b = project(o_gb, d_model)
    g_c = project(o_gc, d_model)

    pos = seq_tile * tm + lax.broadcasted_iota(jnp.int32, (tm, LANES), 0)
    ya_parts = []
    for g, w in enumerate(POOL_WINDOWS):
        cur = a_s[g, POOL_HALO:POOL_HALO + tm, :]
        s = cur
        for k in range(1, w):
            s = s + a_s[g, POOL_HALO - k:POOL_HALO - k + tm, :]
        cnt = jnp.minimum(pos + 1, w).astype(f32)
        pa = s / cnt - cur
        ya_parts.append(_dot(pa.astype(bf16), wpool_ref[g]))
    a_s[:, 0:POOL_HALO, :] = a_s[:, tm:tm + POOL_HALO, :]
    ya = (jnp.concatenate(ya_parts, axis=1) * pscale_ref[...]).astype(bf16)

    u = _gelu_tanh(u)
    vn = _rmsnorm(_gelu_tanh(v), gsgu_ref[...]).astype(bf16)
    row = lax.broadcasted_iota(jnp.int32, (CHUNK, CHUNK), 0)
    col = lax.broadcasted_iota(jnp.int32, (CHUNK, CHUNK), 1)
    head_dim = wb // SGU_HEADS
    for g in range(SGU_HEADS):
        ws_g = jnp.where(row >= col, ws_ref[g], 0.0).astype(bf16)
        bias_g = bsp_ref[g]
        lanes = slice(g * head_dim, (g + 1) * head_dim)
        for c in range(tm // CHUNK):
            rows = slice(c * CHUNK, (c + 1) * CHUNK)
            sv = _dot(ws_g, vn[rows, lanes]) + bias_g
            yb_s[rows, lanes] = (u[rows, lanes] * sv).astype(bf16)

    _store_lane_tiles(z_s, CONV_HALO, c_c * c_x)
    conv_parts = []
    for t in range(wc // LANES):
        lc = slice(t * LANES, (t + 1) * LANES)
        conv = z_s[t, CONV_HALO:CONV_HALO + tm, :] * convc_ref[CONV_WIDTH - 1:CONV_WIDTH, lc]
        for k in range(1, CONV_WIDTH):
            conv = conv + (z_s[t, CONV_HALO - k:CONV_HALO - k + tm, :]
                           * convc_ref[CONV_WIDTH - 1 - k:CONV_WIDTH - k, lc])
        conv_parts.append(conv)
    z_s[:, 0:CONV_HALO, :] = z_s[:, tm:tm + CONV_HALO, :]
    yc = (c_b * jnp.concatenate(conv_parts, axis=1)).astype(bf16)

    merged = (_sigmoid(g_a) * _dot(ya, wba_ref[...])
              + _sigmoid(g_b) * _dot(yb_s[...], wbb_ref[...])
              + _sigmoid(g_c) * _dot(yc, wbc_ref[...]))
    out_ref[...] = x_ref[...] + _dot(merged.astype(bf16), wo_ref[...])


def _ffn_kernel(x_ref, gffn_ref, wup_ref, convw_ref, convb_ref, wdown_ref, gfin_ref, out_ref,
                hb_s, up_s, act_s, acc_s, *, tm, tiles_per_seq, final_norm):
    f32, bf16 = jnp.float32, jnp.bfloat16
    d_ff = wdown_ref.shape[0]
    seq_tile = lax.rem(pl.program_id(0), tiles_per_seq)

    @pl.when(seq_tile == 0)
    def _():
        up_s[:, 0:CONV_HALO, :] = jnp.zeros((2 * d_ff // LANES, CONV_HALO, LANES), f32)

    hb_s[...] = _rmsnorm(x_ref[...], gffn_ref[...]).astype(bf16)

    chunks = [(lo, min(lo + FF_CHUNK, d_ff)) for lo in range(0, d_ff, FF_CHUNK)]

    def up_project(j):
        lo, hi = chunks[j]
        for base in (lo, d_ff + lo):
            tiles = up_s.at[base // LANES:(base + hi - lo) // LANES]
            _store_lane_tiles(tiles, CONV_HALO, _dot(hb_s[...], wup_ref[:, base:base + hi - lo]))

    def conv3(cols):
        parts = []
        for t in range(cols.start // LANES, cols.stop // LANES):
            lc = slice(t * LANES, (t + 1) * LANES)
            out = (up_s[t, CONV_HALO:CONV_HALO + tm, :] * convw_ref[CONV_WIDTH - 1:CONV_WIDTH, lc]
                   + convb_ref[:, lc])
            for k in range(1, CONV_WIDTH):
                out = out + (up_s[t, CONV_HALO - k:CONV_HALO - k + tm, :]
                             * convw_ref[CONV_WIDTH - 1 - k:CONV_WIDTH - k, lc])
            parts.append(out)
        return jnp.concatenate(parts, axis=1)

    def activate(j):
        lo, hi = chunks[j]
        half_gate = 0.5 * conv3(slice(lo, hi))
        value = conv3(slice(d_ff + lo, d_ff + hi))
        act_s[:, lo:hi] = (half_gate * (1.0 + jnp.tanh(half_gate)) * value).astype(bf16)

    def down_project(j):
        lo, hi = chunks[j]
        down = _dot(act_s[:, lo:hi], wdown_ref[lo:hi, :])
        if j == 0:
            acc_s[...] = x_ref[...] + down
        else:
            acc_s[...] += down

    for j in range(len(chunks)):
        up_project(j)
    for j in range(len(chunks)):
        activate(j)
        down_project(j)
    up_s[:, 0:CONV_HALO, :] = up_s[:, tm:tm + CONV_HALO, :]

    y = acc_s[...]
    if final_norm:
        y = _rmsnorm(y, gfin_ref[...])
    out_ref[...] = y


def _full(shape):
    return pl.BlockSpec(shape, lambda i: (0,) * len(shape))


def _layer(shape, layer):
    return pl.BlockSpec((None,) + tuple(shape[1:]), lambda i: (layer,) + (0,) * (len(shape) - 1),
                        pipeline_mode=pl.Buffered(1))


def _mixer_call(x2d, gmix, win, wpool, pscale, gsgu, ws, bsp, convc, wba, wbb, wbc, wo,
                *, layer, tm, tiles_per_seq):
    t, d = x2d.shape
    width = wba.shape[1]
    kern = functools.partial(_mixer_kernel, tm=tm, tiles_per_seq=tiles_per_seq, width=width)
    row_spec = pl.BlockSpec((tm, d), lambda i: (i, 0))
    return pl.pallas_call(
        kern,
        grid=(t // tm,),
        in_specs=[row_spec, _full(gmix.shape), _layer(win.shape, layer), _full(wpool.shape),
                  _full(pscale.shape), _full(gsgu.shape), _full(ws.shape), _full(bsp.shape),
                  _full(convc.shape), _layer(wba.shape, layer), _layer(wbb.shape, layer),
                  _layer(wbc.shape, layer), _layer(wo.shape, layer)],
        out_specs=row_spec,
        out_shape=jax.ShapeDtypeStruct((t, d), jnp.float32),
        scratch_shapes=[
            pltpu.VMEM((tm, d), jnp.bfloat16),
            pltpu.VMEM((width // LANES, POOL_HALO + tm, LANES), jnp.float32),
            pltpu.VMEM((width // LANES, CONV_HALO + tm, LANES), jnp.float32),
            pltpu.VMEM((tm, width), jnp.bfloat16),
        ],
        compiler_params=pltpu.CompilerParams(
            dimension_semantics=("arbitrary",), vmem_limit_bytes=VMEM_LIMIT_BYTES),
        name="mixer",
    )(x2d, gmix, win, wpool, pscale, gsgu, ws, bsp, convc, wba, wbb, wbc, wo)


def _ffn_call(x2d, gffn, wup, convw, convb, wdown, gfin,
              *, layer, tm, tiles_per_seq, final_norm):
    t, d = x2d.shape
    d_ff = wdown.shape[1]
    kern = functools.partial(_ffn_kernel, tm=tm, tiles_per_seq=tiles_per_seq,
                             final_norm=final_norm)
    row_spec = pl.BlockSpec((tm, d), lambda i: (i, 0))
    return pl.pallas_call(
        kern,
        grid=(t // tm,),
        in_specs=[row_spec, _full(gffn.shape), _layer(wup.shape, layer), _full(convw.shape),
                  _full(convb.shape), _layer(wdown.shape, layer), _full(gfin.shape)],
        out_specs=row_spec,
        out_shape=jax.ShapeDtypeStruct((t, d), jnp.float32),
        scratch_shapes=[
            pltpu.VMEM((tm, d), jnp.bfloat16),
            pltpu.VMEM((2 * d_ff // LANES, CONV_HALO + tm, LANES), jnp.float32),
            pltpu.VMEM((tm, d_ff), jnp.bfloat16),
            pltpu.VMEM((tm, d), jnp.float32),
        ],
        compiler_params=pltpu.CompilerParams(
            dimension_semantics=("arbitrary",), vmem_limit_bytes=VMEM_LIMIT_BYTES),
        name="ffn",
    )(x2d, gffn, wup, convw, convb, wdown, gfin)


def kernel(x, g_mix, w_in, w_pool, pool_scale, g_sgu, w_spatial, b_spatial, conv_c,
           w_branch_a, w_branch_b, w_branch_c, w_o, g_ffn, w_up, conv_ffn, conv_ffn_b,
           w_down, g_final):
    bsz, seq, d = x.shape
    depth = g_mix.shape[0]
    d_ff = w_down.shape[1]
    for tm in (MIXER_TILE, FFN_TILE):
        assert seq % tm == 0 and tm % CHUNK == 0
    assert d_ff % LANES == 0
    assert w_spatial.shape[2] == CHUNK and w_pool.shape[1] == len(POOL_WINDOWS)
    bf16 = jnp.bfloat16

    win, wba, wbb, wbc, wo, wup, wdown = (
        w.astype(bf16) for w in (w_in, w_branch_a, w_branch_b, w_branch_c, w_o, w_up, w_down))

    h = x.reshape(bsz * seq, d)
    for l in range(depth):
        bsp = jnp.broadcast_to(b_spatial[l][:, :, None], (SGU_HEADS, CHUNK, LANES))
        h = _mixer_call(
            h, g_mix[l][None], win, w_pool[l].astype(bf16), pool_scale[l][None],
            g_sgu[l][None], w_spatial[l], bsp, conv_c[l], wba, wbb, wbc, wo,
            layer=l, tm=MIXER_TILE, tiles_per_seq=seq // MIXER_TILE)
        h = _ffn_call(
            h, g_ffn[l][None], wup, conv_ffn[l], conv_ffn_b[l][None], wdown, g_final[None],
            layer=l, tm=FFN_TILE, tiles_per_seq=seq // FFN_TILE, final_norm=(l == depth - 1))
    return h.reshape(bsz, seq, d)
```

```python
import functools

import jax
import jax.numpy as jnp
from jax import lax
from jax.experimental import pallas as pl
from jax.experimental.pallas import tpu as pltpu

POOL_WINDOWS = (2, 4, 8, 16)
CHUNK = 128
SGU_HEADS = 4
CONV_WIDTH = 3
EPS = 1e-6

LANES = 128
SUBLANES = 8
POOL_HALO = 16
CONV_HALO = SUBLANES
FF_CHUNK = 512

V7X_VMEM_BYTES = 64 * 1024 * 1024
VMEM_LIMIT_BYTES = V7X_VMEM_BYTES - 2 * 1024 * 1024

MIXER_TILE = 1024
FFN_TILE = 1024
FFN_ROW_BLOCK = 512


def _dot(a, b):
    return jnp.dot(a, b, preferred_element_type=jnp.float32)


def _store_lane_tiles(ref, row0, value):
    rows = value.shape[0]
    for t in range(value.shape[1] // LANES):
        ref[t, row0:row0 + rows, :] = value[:, t * LANES:(t + 1) * LANES]


def _rmsnorm(x, g):
    ms = jnp.mean(x * x, axis=-1, keepdims=True)
    return x * lax.rsqrt(ms + EPS) * g


def _sigmoid(x):
    return 0.5 * jnp.tanh(0.5 * x) + 0.5


def _gelu_tanh(x):
    c = 0.7978845608028654
    return x * (0.5 * (1.0 + jnp.tanh(c * (x + 0.044715 * (x * x * x)))))


def _mixer_kernel(x_ref, gmix_ref, win_ref, wpool_ref, pscale_ref, gsgu_ref, ws_ref, bsp_ref,
                  convc_ref, wba_ref, wbb_ref, wbc_ref, wo_ref, out_ref,
                  hb_s, a_s, z_s, yb_s, *, tm, tiles_per_seq, width):
    f32, bf16 = jnp.float32, jnp.bfloat16
    wa = wb = wc = width
    d_model = x_ref.shape[1]
    seq_tile = lax.rem(pl.program_id(0), tiles_per_seq)

    @pl.when(seq_tile == 0)
    def _():
        a_s[:, 0:POOL_HALO, :] = jnp.zeros((wa // LANES, POOL_HALO, LANES), f32)
        z_s[:, 0:CONV_HALO, :] = jnp.zeros((wc // LANES, CONV_HALO, LANES), f32)

    hb_s[...] = _rmsnorm(x_ref[...], gmix_ref[...]).astype(bf16)

    o_a, o_u, o_v = 0, wa, wa + wb
    o_cb = wa + 2 * wb
    o_cc, o_cx = o_cb + wc, o_cb + 2 * wc
    o_ga = o_cb + 3 * wc
    o_gb, o_gc = o_ga + d_model, o_ga + 2 * d_model

    def project(offset, n):
        return _dot(hb_s[...], win_ref[:, offset:offset + n])

    _store_lane_tiles(a_s, POOL_HALO, project(o_a, wa))
    u = project(o_u, wb)
    v = project(o_v, wb)
    c_c = project(o_cc, wc)
    c_x = project(o_cx, wc)
    c_b = project(o_cb, wc)
    g_a = project(o_ga, d_model)
    g_b = project(o_gb, d_model)
    g_c = project(o_gc, d_model)

    pos = seq_tile * tm + lax.broadcasted_iota(jnp.int32, (tm, LANES), 0)
    ya_parts = []
    for g, w in enumerate(POOL_WINDOWS):
        cur = a_s[g, POOL_HALO:POOL_HALO + tm, :]
        s = cur
        for k in range(1, w):
            s = s + a_s[g, POOL_HALO - k:POOL_HALO - k + tm, :]
        cnt = jnp.minimum(pos + 1, w).astype(f32)
        pa = s / cnt - cur
        ya_parts.append(_dot(pa.astype(bf16), wpool_ref[g]))
    a_s[:, 0:POOL_HALO, :] = a_s[:, tm:tm + POOL_HALO, :]
    ya = (jnp.concatenate(ya_parts, axis=1) * pscale_ref[...]).astype(bf16)

    u = _gelu_tanh(u)
    vn = _rmsnorm(_gelu_tanh(v), gsgu_ref[...]).astype(bf16)
    row = lax.broadcasted_iota(jnp.int32, (CHUNK, CHUNK), 0)
    col = lax.broadcasted_iota(jnp.int32, (CHUNK, CHUNK), 1)
    head_dim = wb // SGU_HEADS
    for g in range(SGU_HEADS):
        ws_g = jnp.where(row >= col, ws_ref[g], 0.0).astype(bf16)
        bias_g = bsp_ref[g]
        lanes = slice(g * head_dim, (g + 1) * head_dim)
        for c in range(tm // CHUNK):
            rows = slice(c * CHUNK, (c + 1) * CHUNK)
            sv = _dot(ws_g, vn[rows, lanes]) + bias_g
            yb_s[rows, lanes] = (u[rows, lanes] * sv).astype(bf16)

    _store_lane_tiles(z_s, CONV_HALO, c_c * c_x)
    conv_parts = []
    for t in range(wc // LANES):
        lc = slice(t * LANES, (t + 1) * LANES)
        conv = z_s[t, CONV_HALO:CONV_HALO + tm, :] * convc_ref[CONV_WIDTH - 1:CONV_WIDTH, lc]
        for k in range(1, CONV_WIDTH):
            conv = conv + (z_s[t, CONV_HALO - k:CONV_HALO - k + tm, :]
                           * convc_ref[CONV_WIDTH - 1 - k:CONV_WIDTH - k, lc])
        conv_parts.append(conv)
    z_s[:, 0:CONV_HALO, :] = z_s[:, tm:tm + CONV_HALO, :]
    yc = (c_b * jnp.concatenate(conv_parts, axis=1)).astype(bf16)

    merged = (_sigmoid(g_a) * _dot(ya, wba_ref[...])
              + _sigmoid(g_b) * _dot(yb_s[...], wbb_ref[...])
              + _sigmoid(g_c) * _dot(yc, wbc_ref[...]))
    out_ref[...] = x_ref[...] + _dot(merged.astype(bf16), wo_ref[...])


def _ffn_kernel(x_ref, gffn_ref, wup_ref, convw_ref, convb_ref, wdown_ref, gfin_ref, out_ref,
                hb_s, up_s, act_s, acc_s, *, tm, sub, tiles_per_seq, final_norm):
    f32, bf16 = jnp.float32, jnp.bfloat16
    d_ff = wdown_ref.shape[0]
    seq_tile = lax.rem(pl.program_id(0), tiles_per_seq)

    @pl.when(seq_tile == 0)
    def _():
        up_s[:, 0:CONV_HALO, :] = jnp.zeros((2 * d_ff // LANES, CONV_HALO, LANES), f32)

    chunks = [(lo, min(lo + FF_CHUNK, d_ff)) for lo in range(0, d_ff, FF_CHUNK)]

    def conv3(cols):
        parts = []
        for t in range(cols.start // LANES, cols.stop // LANES):
            lc = slice(t * LANES, (t + 1) * LANES)
            out = (up_s[t, CONV_HALO:CONV_HALO + sub, :] * convw_ref[CONV_WIDTH - 1:CONV_WIDTH, lc]
                   + convb_ref[:, lc])
            for k in range(1, CONV_WIDTH):
                out = out + (up_s[t, CONV_HALO - k:CONV_HALO - k + sub, :]
                             * convw_ref[CONV_WIDTH - 1 - k:CONV_WIDTH - k, lc])
            parts.append(out)
        return jnp.concatenate(parts, axis=1)

    for r0 in range(0, tm, sub):
        rows = slice(r0, r0 + sub)
        hb_s[...] = _rmsnorm(x_ref[rows, :], gffn_ref[...]).astype(bf16)

        for lo, hi in chunks:
            for base in (lo, d_ff + lo):
                tiles = up_s.at[base // LANES:(base + hi - lo) // LANES]
                _store_lane_tiles(tiles, CONV_HALO,
                                  _dot(hb_s[...], wup_ref[:, base:base + hi - lo]))
        for j, (lo, hi) in enumerate(chunks):
            half_gate = 0.5 * conv3(slice(lo, hi))
            value = conv3(slice(d_ff + lo, d_ff + hi))
            act_s[:, lo:hi] = (half_gate * (1.0 + jnp.tanh(half_gate)) * value).astype(bf16)
            down = _dot(act_s[:, lo:hi], wdown_ref[lo:hi, :])
            if j == 0:
                acc_s[...] = x_ref[rows, :] + down
            else:
                acc_s[...] += down
        up_s[:, 0:CONV_HALO, :] = up_s[:, sub:sub + CONV_HALO, :]

        y = acc_s[...]
        if final_norm:
            y = _rmsnorm(y, gfin_ref[...])
        out_ref[rows, :] = y


def _full(shape):
    return pl.BlockSpec(shape, lambda i: (0,) * len(shape))


def _layer(shape, layer):
    return pl.BlockSpec((None,) + tuple(shape[1:]), lambda i: (layer,) + (0,) * (len(shape) - 1),
                        pipeline_mode=pl.Buffered(1))


def _mixer_call(x2d, gmix, win, wpool, pscale, gsgu, ws, bsp, convc, wba, wbb, wbc, wo,
                *, layer, tm, tiles_per_seq):
    t, d = x2d.shape
    width = wba.shape[1]
    kern = functools.partial(_mixer_kernel, tm=tm, tiles_per_seq=tiles_per_seq, width=width)
    row_spec = pl.BlockSpec((tm, d), lambda i: (i, 0))
    return pl.pallas_call(
        kern,
        grid=(t // tm,),
        in_specs=[row_spec, _full(gmix.shape), _layer(win.shape, layer), _full(wpool.shape),
                  _full(pscale.shape), _full(gsgu.shape), _full(ws.shape), _full(bsp.shape),
                  _full(convc.shape), _layer(wba.shape, layer), _layer(wbb.shape, layer),
                  _layer(wbc.shape, layer), _layer(wo.shape, layer)],
        out_specs=row_spec,
        out_shape=jax.ShapeDtypeStruct((t, d), jnp.float32),
        scratch_shapes=[
            pltpu.VMEM((tm, d), jnp.bfloat16),
            pltpu.VMEM((width // LANES, POOL_HALO + tm, LANES), jnp.float32),
            pltpu.VMEM((width // LANES, CONV_HALO + tm, LANES), jnp.float32),
            pltpu.VMEM((tm, width), jnp.bfloat16),
        ],
        compiler_params=pltpu.CompilerParams(
            dimension_semantics=("arbitrary",), vmem_limit_bytes=VMEM_LIMIT_BYTES),
        name="mixer",
    )(x2d, gmix, win, wpool, pscale, gsgu, ws, bsp, convc, wba, wbb, wbc, wo)


def _ffn_call(x2d, gffn, wup, convw, convb, wdown, gfin,
              *, layer, tm, tiles_per_seq, final_norm):
    t, d = x2d.shape
    d_ff = wdown.shape[1]
    sub = FFN_ROW_BLOCK
    kern = functools.partial(_ffn_kernel, tm=tm, sub=sub, tiles_per_seq=tiles_per_seq,
                             final_norm=final_norm)
    row_spec = pl.BlockSpec((tm, d), lambda i: (i, 0))
    return pl.pallas_call(
        kern,
        grid=(t // tm,),
        in_specs=[row_spec, _full(gffn.shape), _layer(wup.shape, layer), _full(convw.shape),
                  _full(convb.shape), _layer(wdown.shape, layer), _full(gfin.shape)],
        out_specs=row_spec,
        out_shape=jax.ShapeDtypeStruct((t, d), jnp.float32),
        scratch_shapes=[
            pltpu.VMEM((sub, d), jnp.bfloat16),
            pltpu.VMEM((2 * d_ff // LANES, CONV_HALO + sub, LANES), jnp.float32),
            pltpu.VMEM((sub, d_ff), jnp.bfloat16),
            pltpu.VMEM((sub, d), jnp.float32),
        ],
        compiler_params=pltpu.CompilerParams(
            dimension_semantics=("arbitrary",), vmem_limit_bytes=VMEM_LIMIT_BYTES),
        name="ffn",
    )(x2d, gffn, wup, convw, convb, wdown, gfin)


def kernel(x, g_mix, w_in, w_pool, pool_scale, g_sgu, w_spatial, b_spatial, conv_c,
           w_branch_a, w_branch_b, w_branch_c, w_o, g_ffn, w_up, conv_ffn, conv_ffn_b,
           w_down, g_final):
    bsz, seq, d = x.shape
    depth = g_mix.shape[0]
    d_ff = w_down.shape[1]
    for tm in (MIXER_TILE, FFN_TILE):
        assert seq % tm == 0 and tm % CHUNK == 0
    assert FFN_TILE % FFN_ROW_BLOCK == 0
    assert d_ff % LANES == 0
    assert w_spatial.shape[2] == CHUNK and w_pool.shape[1] == len(POOL_WINDOWS)
    bf16 = jnp.bfloat16

    win, wba, wbb, wbc, wo, wup, wdown = (
        w.astype(bf16) for w in (w_in, w_branch_a, w_branch_b, w_branch_c, w_o, w_up, w_down))

    h = x.reshape(bsz * seq, d)
    for l in range(depth):
        bsp = jnp.broadcast_to(b_spatial[l][:, :, None], (SGU_HEADS, CHUNK, LANES))
        h = _mixer_call(
            h, g_mix[l][None], win, w_pool[l].astype(bf16), pool_scale[l][None],
            g_sgu[l][None], w_spatial[l], bsp, conv_c[l], wba, wbb, wbc, wo,
            layer=l, tm=MIXER_TILE, tiles_per_seq=seq // MIXER_TILE)
        h = _ffn_call(
            h, g_ffn[l][None], wup, conv_ffn[l], conv_ffn_b[l][None], wdown, g_final[None],
            layer=l, tm=FFN_TILE, tiles_per_seq=seq // FFN_TILE, final_norm=(l == depth - 1))
    return h.reshape(bsz, seq, d)
```

```python
import functools

import jax
import jax.numpy as jnp
from jax import lax
from jax.experimental import pallas as pl
from jax.experimental.pallas import tpu as pltpu

POOL_WINDOWS = (2, 4, 8, 16)
CHUNK = 128
SGU_HEADS = 4
CONV_WIDTH = 3
EPS = 1e-6

LANES = 128
SUBLANES = 8
POOL_HALO = 16
CONV_HALO = SUBLANES
FF_CHUNK = 512

V7X_VMEM_BYTES = 64 * 1024 * 1024
VMEM_LIMIT_BYTES = V7X_VMEM_BYTES - 8 * 1024 * 1024

MIXER_TILE = 1024
MIXER_ROW_BLOCK = 512
FFN_TILE = 1024
FFN_ROW_BLOCK = 512


def _dot(a, b):
    return jnp.dot(a, b, preferred_element_type=jnp.float32)


def _store_lane_tiles(ref, row0, value):
    rows = value.shape[0]
    for t in range(value.shape[1] // LANES):
        ref[t, row0:row0 + rows, :] = value[:, t * LANES:(t + 1) * LANES]


def _rmsnorm(x, g):
    ms = jnp.mean(x * x, axis=-1, keepdims=True)
    return x * lax.rsqrt(ms + EPS) * g


def _sigmoid(x):
    return 0.5 * jnp.tanh(0.5 * x) + 0.5


def _gelu_tanh(x):
    c = 0.7978845608028654
    return x * (0.5 * (1.0 + jnp.tanh(c * (x + 0.044715 * (x * x * x)))))


def _mixer_kernel(x_ref, gmix_ref, win_ref, wpool_ref, pscale_ref, gsgu_ref, ws_ref, bsp_ref,
                  convc_ref, wba_ref, wbb_ref, wbc_ref, wo_ref, out_ref,
                  hb_s, a_s, z_s, yb_s, *, tm, sub, tiles_per_seq, width):
    f32, bf16 = jnp.float32, jnp.bfloat16
    wa = wb = wc = width
    d_model = x_ref.shape[1]
    seq_tile = lax.rem(pl.program_id(0), tiles_per_seq)

    @pl.when(seq_tile == 0)
    def _():
        a_s[:, 0:POOL_HALO, :] = jnp.zeros((wa // LANES, POOL_HALO, LANES), f32)
        z_s[:, 0:CONV_HALO, :] = jnp.zeros((wc // LANES, CONV_HALO, LANES), f32)

    o_a, o_u, o_v = 0, wa, wa + wb
    o_cb = wa + 2 * wb
    o_cc, o_cx = o_cb + wc, o_cb + 2 * wc
    o_ga = o_cb + 3 * wc
    o_gb, o_gc = o_ga + d_model, o_ga + 2 * d_model

    for r0 in range(0, tm, sub):
        rows = slice(r0, r0 + sub)
        hb_s[...] = _rmsnorm(x_ref[rows, :], gmix_ref[...]).astype(bf16)

        def project(offset, n):
            return _dot(hb_s[...], win_ref[:, offset:offset + n])

        _store_lane_tiles(a_s, POOL_HALO, project(o_a, wa))
        u = project(o_u, wb)
        v = project(o_v, wb)
        c_c = project(o_cc, wc)
        c_x = project(o_cx, wc)
        c_b = project(o_cb, wc)
        g_a = project(o_ga, d_model)
        g_b = project(o_gb, d_model)
        g_c = project(o_gc, d_model)

        pos = seq_tile * tm + r0 + lax.broadcasted_iota(jnp.int32, (sub, LANES), 0)
        ya_parts = []
        for g, w in enumerate(POOL_WINDOWS):
            cur = a_s[g, POOL_HALO:POOL_HALO + sub, :]
            s = cur
            for k in range(1, w):
                s = s + a_s[g, POOL_HALO - k:POOL_HALO - k + sub, :]
            cnt = jnp.minimum(pos + 1, w).astype(f32)
            pa = s / cnt - cur
            ya_parts.append(_dot(pa.astype(bf16), wpool_ref[g]))
        a_s[:, 0:POOL_HALO, :] = a_s[:, sub:sub + POOL_HALO, :]
        ya = (jnp.concatenate(ya_parts, axis=1) * pscale_ref[...]).astype(bf16)

        u = _gelu_tanh(u)
        vn = _rmsnorm(_gelu_tanh(v), gsgu_ref[...]).astype(bf16)
        row = lax.broadcasted_iota(jnp.int32, (CHUNK, CHUNK), 0)
        col = lax.broadcasted_iota(jnp.int32, (CHUNK, CHUNK), 1)
        head_dim = wb // SGU_HEADS
        for g in range(SGU_HEADS):
            ws_g = jnp.where(row >= col, ws_ref[g], 0.0).astype(bf16)
            bias_g = bsp_ref[g]
            lanes = slice(g * head_dim, (g + 1) * head_dim)
            for c in range(sub // CHUNK):
                crows = slice(c * CHUNK, (c + 1) * CHUNK)
                sv = _dot(ws_g, vn[crows, lanes]) + bias_g
                yb_s[crows, lanes] = (u[crows, lanes] * sv).astype(bf16)

        _store_lane_tiles(z_s, CONV_HALO, c_c * c_x)
        conv_parts = []
        for t in range(wc // LANES):
            lc = slice(t * LANES, (t + 1) * LANES)
            conv = z_s[t, CONV_HALO:CONV_HALO + sub, :] * convc_ref[CONV_WIDTH - 1:CONV_WIDTH, lc]
            for k in range(1, CONV_WIDTH):
                conv = conv + (z_s[t, CONV_HALO - k:CONV_HALO - k + sub, :]
                               * convc_ref[CONV_WIDTH - 1 - k:CONV_WIDTH - k, lc])
            conv_parts.append(conv)
        z_s[:, 0:CONV_HALO, :] = z_s[:, sub:sub + CONV_HALO, :]
        yc = (c_b * jnp.concatenate(conv_parts, axis=1)).astype(bf16)

        merged = (_sigmoid(g_a) * _dot(ya, wba_ref[...])
                  + _sigmoid(g_b) * _dot(yb_s[...], wbb_ref[...])
                  + _sigmoid(g_c) * _dot(yc, wbc_ref[...]))
        out_ref[rows, :] = x_ref[rows, :] + _dot(merged.astype(bf16), wo_ref[...])


def _ffn_kernel(x_ref, gffn_ref, wup_ref, convw_ref, convb_ref, wdown_ref, gfin_ref, out_ref,
                hb_s, up_s, act_s, acc_s, *, tm, sub, tiles_per_seq, final_norm):
    f32, bf16 = jnp.float32, jnp.bfloat16
    d_ff = wdown_ref.shape[0]
    seq_tile = lax.rem(pl.program_id(0), tiles_per_seq)

    @pl.when(seq_tile == 0)
    def _():
        up_s[:, 0:CONV_HALO, :] = jnp.zeros((2 * d_ff // LANES, CONV_HALO, LANES), f32)

    chunks = [(lo, min(lo + FF_CHUNK, d_ff)) for lo in range(0, d_ff, FF_CHUNK)]

    def conv3(cols):
        parts = []
        for t in range(cols.start // LANES, cols.stop // LANES):
            lc = slice(t * LANES, (t + 1) * LANES)
            out = (up_s[t, CONV_HALO:CONV_HALO + sub, :] * convw_ref[CONV_WIDTH - 1:CONV_WIDTH, lc]
                   + convb_ref[:, lc])
            for k in range(1, CONV_WIDTH):
                out = out + (up_s[t, CONV_HALO - k:CONV_HALO - k + sub, :]
                             * convw_ref[CONV_WIDTH - 1 - k:CONV_WIDTH - k, lc])
            parts.append(out)
        return jnp.concatenate(parts, axis=1)

    for r0 in range(0, tm, sub):
        rows = slice(r0, r0 + sub)
        hb_s[...] = _rmsnorm(x_ref[rows, :], gffn_ref[...]).astype(bf16)

        for lo, hi in chunks:
            for base in (lo, d_ff + lo):
                tiles = up_s.at[base // LANES:(base + hi - lo) // LANES]
                _store_lane_tiles(tiles, CONV_HALO,
                                  _dot(hb_s[...], wup_ref[:, base:base + hi - lo]))
        for j, (lo, hi) in enumerate(chunks):
            half_gate = 0.5 * conv3(slice(lo, hi))
            value = conv3(slice(d_ff + lo, d_ff + hi))
            act_s[:, lo:hi] = (half_gate * (1.0 + jnp.tanh(half_gate)) * value).astype(bf16)
            down = _dot(act_s[:, lo:hi], wdown_ref[lo:hi, :])
            if j == 0:
                acc_s[...] = x_ref[rows, :] + down
            else:
                acc_s[...] += down
        up_s[:, 0:CONV_HALO, :] = up_s[:, sub:sub + CONV_HALO, :]

        y = acc_s[...]
        if final_norm:
            y = _rmsnorm(y, gfin_ref[...])
        out_ref[rows, :] = y


def _full(shape):
    return pl.BlockSpec(shape, lambda i: (0,) * len(shape))


def _layer(shape, layer):
    return pl.BlockSpec((None,) + tuple(shape[1:]), lambda i: (layer,) + (0,) * (len(shape) - 1),
                        pipeline_mode=pl.Buffered(1))


def _mixer_call(x2d, gmix, win, wpool, pscale, gsgu, ws, bsp, convc, wba, wbb, wbc, wo,
                *, layer, tm, tiles_per_seq):
    t, d = x2d.shape
    width = wba.shape[1]
    sub = MIXER_ROW_BLOCK
    kern = functools.partial(_mixer_kernel, tm=tm, sub=sub, tiles_per_seq=tiles_per_seq,
                             width=width)
    row_spec = pl.BlockSpec((tm, d), lambda i: (i, 0))
    return pl.pallas_call(
        kern,
        grid=(t // tm,),
        in_specs=[row_spec, _full(gmix.shape), _layer(win.shape, layer), _full(wpool.shape),
                  _full(pscale.shape), _full(gsgu.shape), _full(ws.shape), _full(bsp.shape),
                  _full(convc.shape), _layer(wba.shape, layer), _layer(wbb.shape, layer),
                  _layer(wbc.shape, layer), _layer(wo.shape, layer)],
        out_specs=row_spec,
        out_shape=jax.ShapeDtypeStruct((t, d), jnp.float32),
        scratch_shapes=[
            pltpu.VMEM((sub, d), jnp.bfloat16),
            pltpu.VMEM((width // LANES, POOL_HALO + sub, LANES), jnp.float32),
            pltpu.VMEM((width // LANES, CONV_HALO + sub, LANES), jnp.float32),
            pltpu.VMEM((sub, width), jnp.bfloat16),
        ],
        compiler_params=pltpu.CompilerParams(
            dimension_semantics=("arbitrary",), vmem_limit_bytes=VMEM_LIMIT_BYTES),
        name="mixer",
    )(x2d, gmix, win, wpool, pscale, gsgu, ws, bsp, convc, wba, wbb, wbc, wo)


def _ffn_call(x2d, gffn, wup, convw, convb, wdown, gfin,
              *, layer, tm, tiles_per_seq, final_norm):
    t, d = x2d.shape
    d_ff = wdown.shape[1]
    sub = FFN_ROW_BLOCK
    kern = functools.partial(_ffn_kernel, tm=tm, sub=sub, tiles_per_seq=tiles_per_seq,
                             final_norm=final_norm)
    row_spec = pl.BlockSpec((tm, d), lambda i: (i, 0))
    return pl.pallas_call(
        kern,
        grid=(t // tm,),
        in_specs=[row_spec, _full(gffn.shape), _layer(wup.shape, layer), _full(convw.shape),
                  _full(convb.shape), _layer(wdown.shape, layer), _full(gfin.shape)],
        out_specs=row_spec,
        out_shape=jax.ShapeDtypeStruct((t, d), jnp.float32),
        scratch_shapes=[
            pltpu.VMEM((sub, d), jnp.bfloat16),
            pltpu.VMEM((2 * d_ff // LANES, CONV_HALO + sub, LANES), jnp.float32),
            pltpu.VMEM((sub, d_ff), jnp.bfloat16),
            pltpu.VMEM((sub, d), jnp.float32),
        ],
        compiler_params=pltpu.CompilerParams(
            dimension_semantics=("arbitrary",), vmem_limit_bytes=VMEM_LIMIT_BYTES),
        name="ffn",
    )(x2d, gffn, wup, convw, convb, wdown, gfin)


def kernel(x, g_mix, w_in, w_pool, pool_scale, g_sgu, w_spatial, b_spatial, conv_c,
           w_branch_a, w_branch_b, w_branch_c, w_o, g_ffn, w_up, conv_ffn, conv_ffn_b,
           w_down, g_final):
    bsz, seq, d = x.shape
    depth = g_mix.shape[0]
    d_ff = w_down.shape[1]
    for tm in (MIXER_TILE, FFN_TILE):
        assert seq % tm == 0 and tm % CHUNK == 0
    assert MIXER_TILE % MIXER_ROW_BLOCK == 0 and MIXER_ROW_BLOCK % CHUNK == 0
    assert FFN_TILE % FFN_ROW_BLOCK == 0
    assert d_ff % LANES == 0
    assert w_spatial.shape[2] == CHUNK and w_pool.shape[1] == len(POOL_WINDOWS)
    bf16 = jnp.bfloat16

    win, wba, wbb, wbc, wo, wup, wdown = (
        w.astype(bf16) for w in (w_in, w_branch_a, w_branch_b, w_branch_c, w_o, w_up, w_down))

    h = x.reshape(bsz * seq, d)
    for l in range(depth):
        bsp = jnp.broadcast_to(b_spatial[l][:, :, None], (SGU_HEADS, CHUNK, LANES))
        h = _mixer_call(
            h, g_mix[l][None], win, w_pool[l].astype(bf16), pool_scale[l][None],
            g_sgu[l][None], w_spatial[l], bsp, conv_c[l], wba, wbb, wbc, wo,
            layer=l, tm=MIXER_TILE, tiles_per_seq=seq // MIXER_TILE)
        h = _ffn_call(
            h, g_ffn[l][None], wup, conv_ffn[l], conv_ffn_b[l][None], wdown, g_final[None],
            layer=l, tm=FFN_TILE, tiles_per_seq=seq // FFN_TILE, final_norm=(l == depth - 1))
    return h.reshape(bsz, seq, d)
```
